```python
import jax, jax.numpy as jnp
from jax import lax
import numpy as np

D_MODEL = 1024
BATCH = 8
SEQ = 8192
DEPTH = 1

N_META = 16
FOURIER_GROUPS = 4
FOURIER_GROUP_DIM = 128
FOURIER_WIDTH = FOURIER_GROUPS * FOURIER_GROUP_DIM
CONV_HEADS = 8
CONV_HEAD_DIM = 64
CONV_WIDTH = CONV_HEADS * CONV_HEAD_DIM
CONV_TAPS = 3
N_BRANCHES = 2
D_FF = 2816
RMS_EPS = 1e-6
IN_COLS = FOURIER_WIDTH + 3 * CONV_WIDTH + N_BRANCHES * D_MODEL

kernel_name = "hybrid_fourier_shortconv_encoder_block"


def rmsnorm(x, g):
    xf = x.astype(jnp.float32)
    var = jnp.mean(xf * xf, axis=-1, keepdims=True)
    return (xf * lax.rsqrt(var + RMS_EPS) * g.astype(jnp.float32)).astype(x.dtype)


def centred_dwconv(x, w, b):
    half = CONV_TAPS // 2
    L = x.shape[1]
    xp = jnp.pad(x, ((0, 0), (half, half), (0, 0)))
    y = b
    for k in range(CONV_TAPS):
        y = y + xp[:, k:k + L] * w[k]
    return y


def fourier_mix(u):
    bn, L, _ = u.shape
    ug = u.astype(jnp.float32).reshape(bn, L, FOURIER_GROUPS, FOURIER_GROUP_DIM)
    f = jnp.fft.fft2(ug, axes=(1, 3), norm="ortho").real
    return f.reshape(bn, L, FOURIER_WIDTH).astype(u.dtype)


def setup_inputs(seed: int = 0) -> dict:
    key = jax.random.key(seed)
    ks = jax.random.split(key, 20)
    f32 = jnp.float32

    def normal(k, shape, fan_in):
        return jax.random.normal(k, shape, f32) * (fan_in ** -0.5)

    def gain(k, shape):
        return 1.0 + 0.02 * jax.random.normal(k, shape, f32)

    def bias(k, shape):
        return 0.01 * jax.random.normal(k, shape, f32)

    return {
        "x": jax.random.normal(ks[0], (BATCH, SEQ, D_MODEL), f32),
        "meta_tokens": jax.random.normal(ks[1], (N_META, D_MODEL), f32),
        "g_mix_pre": gain(ks[2], (DEPTH, D_MODEL)),
        "w_in": normal(ks[3], (DEPTH, D_MODEL, IN_COLS), D_MODEL),
        "b_gates": bias(ks[4], (DEPTH, N_BRANCHES * D_MODEL)),
        "w_fourier": normal(ks[5], (DEPTH, FOURIER_WIDTH, D_MODEL), FOURIER_WIDTH),
        "conv_w_mix": normal(ks[6], (DEPTH, CONV_TAPS, CONV_WIDTH), CONV_TAPS),
        "conv_b_mix": bias(ks[7], (DEPTH, CONV_WIDTH)),
        "w_conv_out": normal(ks[8], (DEPTH, CONV_WIDTH, D_MODEL), CONV_WIDTH),
        "w_out": normal(ks[9], (DEPTH, D_MODEL, D_MODEL), D_MODEL),
        "g_mix_post": gain(ks[10], (DEPTH, D_MODEL)),
        "g_ffn_pre": gain(ks[11], (DEPTH, D_MODEL)),
        "w_up": normal(ks[12], (DEPTH, D_MODEL, 2 * D_FF), D_MODEL),
        "conv_w_ffn": normal(ks[13], (DEPTH, CONV_TAPS, 2 * D_FF), CONV_TAPS),
        "conv_b_ffn": bias(ks[14], (DEPTH, 2 * D_FF)),
        "w_down": normal(ks[15], (DEPTH, D_FF, D_MODEL), D_FF),
        "g_ffn_post": gain(ks[16], (DEPTH, D_MODEL)),
    }


def reference(x, meta_tokens, g_mix_pre, w_in, b_gates, w_fourier, conv_w_mix, conv_b_mix,
              w_conv_out, w_out, g_mix_post, g_ffn_pre, w_up, conv_w_ffn, conv_b_ffn,
              w_down, g_ffn_post):
    bn = x.shape[0]
    meta = jnp.broadcast_to(meta_tokens[None].astype(x.dtype), (bn, N_META, D_MODEL))
    h_res = jnp.concatenate([meta, x], axis=1)
    L = h_res.shape[1]

    splits = [FOURIER_WIDTH,
              FOURIER_WIDTH + CONV_WIDTH,
              FOURIER_WIDTH + 2 * CONV_WIDTH,
              FOURIER_WIDTH + 3 * CONV_WIDTH]

    for l in range(DEPTH):
        h = rmsnorm(h_res, g_mix_pre[l])
        proj = h @ w_in[l]
        u_a, c_g, b_g, v_b, gate_logits = jnp.split(proj, splits, axis=-1)

        y_a = fourier_mix(u_a) @ w_fourier[l]

        conv_in = c_g * v_b
        y_b = (b_g * centred_dwconv(conv_in, conv_w_mix[l], conv_b_mix[l])) @ w_conv_out[l]

        gates = jax.nn.sigmoid((gate_logits + b_gates[l]).reshape(bn, L, N_BRANCHES, D_MODEL))
        merged = gates[:, :, 0] * y_a + gates[:, :, 1] * y_b
        mix_out = merged @ w_out[l]
        h_res = h_res + rmsnorm(mix_out, g_mix_post[l])

        h2 = rmsnorm(h_res, g_ffn_pre[l])
        up = centred_dwconv(h2 @ w_up[l], conv_w_ffn[l], conv_b_ffn[l])
        a, v = jnp.split(up, 2, axis=-1)
        ffn_out = (jax.nn.gelu(a) * v) @ w_down[l]
        h_res = h_res + rmsnorm(ffn_out, g_ffn_post[l])

    return h_res[:, N_META:]
```

```python
import functools

import numpy as np
import jax
import jax.numpy as jnp
from jax.experimental import pallas as pl
from jax.experimental.pallas import tpu as pltpu

D = 1024
N_META = 16
SEQ = 8192
L = SEQ + N_META
FW = 512
GD = 128
NG = 4
CW = 512
D_FF = 2816
EPS = 1e-6
IN_EXT = 2 * FW + 3 * CW + 2 * D

N1 = 76
N2 = 108
N1P = 80
N2P = 112

TM = 912
HALO = 16
TF = 1024
FH = 32
FC = 256
NFC = D_FF // FC
FR = 128
NP = 4
FP = TF // NP
UP_ROWS = (0, 272, 528, 784, TF + FH)

VMEM_LIMIT = 56 * 1024 * 1024
BF = jnp.bfloat16
F32 = jnp.float32


def _rms(x, g):
    var = jnp.mean(x * x, axis=-1, keepdims=True)
    return x * jax.lax.rsqrt(var + EPS) * g


def _bdot(a, b):
    return jnp.dot(a, b, preferred_element_type=F32)


def _channel_dft_table():
    n = np.arange(GD)
    ang = 2.0 * np.pi * ((n[:, None] * n[None, :]) % GD) / GD
    s = 1.0 / np.sqrt(GD)
    return np.concatenate([np.cos(ang) * s, np.sin(ang) * s], axis=1).astype(np.float32)


def _seq_fft_tables():
    k1 = np.arange(N1)[None, :, None]
    n1 = np.arange(N1)[None, None, :]
    n2 = np.arange(N2)[:, None, None]
    ang = 2.0 * np.pi * ((k1 * (n2 + N2 * n1)) % L) / L
    m1 = np.zeros((N2, 2 * N1P, N1), np.float64)
    m1[:, :N1] = np.cos(ang) / np.sqrt(N1)
    m1[:, N1P:N1P + N1] = np.sin(ang) / np.sqrt(N1)
    k2 = np.arange(N2)[:, None]
    m2 = np.arange(N2)[None, :]
    ang2 = 2.0 * np.pi * ((k2 * m2) % N2) / N2
    c2 = np.cos(ang2) / np.sqrt(N2)
    s2 = np.sin(ang2) / np.sqrt(N2)
    bdc = np.zeros((2 * N2P, 2 * N2), np.float64)
    bds = np.zeros((2 * N2P, 2 * N2), np.float64)
    bdc[:N2, :N2] = c2
    bdc[N2P:N2P + N2, N2:] = c2
    bds[:N2, :N2] = -s2
    bds[N2P:N2P + N2, N2:] = -s2
    return m1.astype(np.float32), bdc.astype(np.float32), bds.astype(np.float32)


def _fold_body(wu_ref, cs_ref, o_ref):
    for g in range(NG):
        o_ref[:, 2 * GD * g:2 * GD * (g + 1)] = jnp.dot(
            wu_ref[:, GD * g:GD * (g + 1)], cs_ref[...],
            preferred_element_type=F32, precision=jax.lax.Precision.HIGHEST)


def _fold(w_u, cs):
    return pl.pallas_call(
        _fold_body,
        out_shape=jax.ShapeDtypeStruct((D, 2 * FW), F32),
        name="fold",
    )(w_u, cs)


def _mix_in_body(xl_ref, xp_ref, xm_ref, meta_ref, g_ref, w_ref, bg_ref, cw_ref, cb_ref, wco_ref,
                 z_ref, g0_ref, mb_ref, hs_ref):
    i = pl.program_id(1)
    g = g_ref[...]
    hs_ref[0:HALO, :] = _rms(xl_ref[0], g).astype(BF)
    first = jnp.where(i == 0, meta_ref[...], xp_ref[0])
    hs_ref[HALO:2 * HALO, :] = _rms(first, g).astype(BF)
    hs_ref[2 * HALO:, :] = _rms(xm_ref[0], g).astype(BF)

    def main():
        return hs_ref[HALO:HALO + TM, :]

    for j in range(NG):
        r = _bdot(main(), w_ref[:, 2 * GD * j:2 * GD * (j + 1)])
        z_ref[0, 2 * j] = r[:, :GD]
        z_ref[0, 2 * j + 1] = r[:, GD:]

    o_c, o_b, o_v, o_ga, o_gb = 2 * FW, 2 * FW + CW, 2 * FW + 2 * CW, 2 * FW + 3 * CW, 2 * FW + 3 * CW + D
    hall = hs_ref[...]
    cin = _bdot(hall, w_ref[:, o_c:o_c + CW]) * _bdot(hall, w_ref[:, o_v:o_v + CW])
    pos = i * TM - HALO + jax.lax.broadcasted_iota(jnp.int32, (TM + 2 * HALO, 1), 0)
    cin = jnp.where((pos >= 0) & (pos < L), cin, 0.0)
    conv = (cb_ref[...]
            + cin[HALO - 1:HALO - 1 + TM] * cw_ref[0:1, :]
            + cin[HALO:HALO + TM] * cw_ref[1:2, :]
            + cin[HALO + 1:HALO + 1 + TM] * cw_ref[2:3, :])
    bgate = _bdot(main(), w_ref[:, o_b:o_b + CW])
    yb = _bdot((bgate * conv).astype(BF), wco_ref[...])

    for h in range(2):
        sl = slice(h * 512, (h + 1) * 512)
        gb = jax.nn.sigmoid(_bdot(main(), w_ref[:, o_gb + h * 512:o_gb + (h + 1) * 512]) + bg_ref[:, D + h * 512:D + (h + 1) * 512])
        mb_ref[0, :, sl] = (gb * yb[:, sl]).astype(BF)
        ga = jax.nn.sigmoid(_bdot(main(), w_ref[:, o_ga + h * 512:o_ga + (h + 1) * 512]) + bg_ref[:, h * 512:(h + 1) * 512])
        g0_ref[0, :, sl] = ga.astype(BF)


def _mix_in(x, meta, g_pre, w_ext, b_gates, cw, cb, wco):
    bn = x.shape[0]
    nt = L // TM
    blk16 = TM // HALO
    const = lambda b, i: (0, 0)
    return pl.pallas_call(
        _mix_in_body,
        grid=(bn, nt),
        in_specs=[
            pl.BlockSpec((1, HALO, D), lambda b, i: (b, jnp.maximum(i * blk16 - 2, 0), 0)),
            pl.BlockSpec((1, HALO, D), lambda b, i: (b, jnp.maximum(i * blk16 - 1, 0), 0)),
            pl.BlockSpec((1, TM, D), lambda b, i: (b, i, 0)),
            pl.BlockSpec((N_META, D), const),
            pl.BlockSpec((1, D), const),
            pl.BlockSpec((D, IN_EXT), const, pipeline_mode=pl.Buffered(1)),
            pl.BlockSpec((1, 2 * D), const),
            pl.BlockSpec((3, CW), const),
            pl.BlockSpec((1, CW), const),
            pl.BlockSpec((CW, D), const, pipeline_mode=pl.Buffered(1)),
        ],
        out_specs=[
            pl.BlockSpec((1, 2 * NG, TM, GD), lambda b, i: (b, 0, i, 0)),
            pl.BlockSpec((1, TM, D), lambda b, i: (b, i, 0)),
            pl.BlockSpec((1, TM, D), lambda b, i: (b, i, 0)),
        ],
        out_shape=[
            jax.ShapeDtypeStruct((bn, 2 * NG, L, GD), F32),
            jax.ShapeDtypeStruct((bn, L, D), BF),
            jax.ShapeDtypeStruct((bn, L, D), BF),
        ],
        scratch_shapes=[pltpu.VMEM((TM + 2 * HALO, D), BF)],
        compiler_params=pltpu.CompilerParams(
            dimension_semantics=("arbitrary", "arbitrary"), vmem_limit_bytes=VMEM_LIMIT),
        name="mix_in",
    )(x, x, x, meta, g_pre, w_ext, b_gates, cw, cb, wco)


def _seqfft_body(z_ref, m1_ref, bdc_ref, bds_ref, y_ref, t_ref):
    def stage1(n2, carry):
        p = z_ref[0, 0, pl.ds(n2, N1, stride=N2), :]
        q = z_ref[0, 1, pl.ds(n2, N1, stride=N2), :]
        d = jnp.concatenate([p, q], axis=1).astype(BF)
        r = _bdot(m1_ref[n2], d)
        t_ref[0, pl.ds(n2, N1, stride=N2), :] = r[0:N1, 0:GD] - r[N1P:N1P + N1, GD:]
        t_ref[1, pl.ds(n2, N1, stride=N2), :] = r[0:N1, GD:] + r[N1P:N1P + N1, 0:GD]
        return carry

    jax.lax.fori_loop(0, N2, stage1, 0, unroll=27)

    def stage2(kp, carry):
        start = pl.multiple_of(kp * (2 * N2), 8)
        tr = t_ref[0, pl.ds(start, 2 * N2), :].astype(BF)
        ti = t_ref[1, pl.ds(start, 2 * N2), :].astype(BF)
        y = _bdot(bdc_ref[...], tr) + _bdot(bds_ref[...], ti)
        y_ref[0, 0, pl.ds(2 * kp, N2, stride=N1), :] = y[0:N2]
        y_ref[0, 0, pl.ds(2 * kp + 1, N2, stride=N1), :] = y[N2P:N2P + N2]
        return carry

    jax.lax.fori_loop(0, N1 // 2, stage2, 0, unroll=19)


def _seqfft(z, m1, bdc, bds):
    bn = z.shape[0]
    return pl.pallas_call(
        _seqfft_body,
        grid=(bn, NG),
        in_specs=[
            pl.BlockSpec((1, 2, L, GD), lambda b, g: (b, g, 0, 0)),
            pl.BlockSpec((N2, 2 * N1P, N1), lambda b, g: (0, 0, 0)),
            pl.BlockSpec((2 * N2P, 2 * N2), lambda b, g: (0, 0)),
            pl.BlockSpec((2 * N2P, 2 * N2), lambda b, g: (0, 0)),
        ],
        out_specs=pl.BlockSpec((1, 1, L, GD), lambda b, g: (b, g, 0, 0)),
        out_shape=jax.ShapeDtypeStruct((bn, NG, L, GD), F32),
        scratch_shapes=[pltpu.VMEM((2, L, GD), F32)],
        compiler_params=pltpu.CompilerParams(
            dimension_semantics=("arbitrary", "arbitrary"), vmem_limit_bytes=VMEM_LIMIT),
        name="seqfft",
    )(z, m1, bdc, bds)


def _mix_out_body(y_ref, g0_ref, mb_ref, xp_ref, xm_ref, meta_ref, wf_ref, wo_ref, gpost_ref, gffn_ref,
                  hres_ref, h2_ref):
    i = pl.program_id(1)
    y = jnp.concatenate([y_ref[0, g] for g in range(NG)], axis=1).astype(BF)
    ya = _bdot(y, wf_ref[...])
    merged = g0_ref[0].astype(F32) * ya + mb_ref[0].astype(F32)
    mix = _bdot(merged.astype(BF), wo_ref[...])
    nrm = _rms(mix, gpost_ref[...])
    first = jnp.where(i == 0, meta_ref[...], xp_ref[0])
    h_a = first + nrm[0:HALO]
    h_b = xm_ref[0, 0:TM - HALO, :] + nrm[HALO:]
    hres_ref[0, 0:HALO, :] = h_a
    hres_ref[0, HALO:, :] = h_b
    h2_ref[0, 0:HALO, :] = _rms(h_a, gffn_ref[...]).astype(BF)
    h2_ref[0, HALO:, :] = _rms(h_b, gffn_ref[...]).astype(BF)


def _mix_out(y, g0, mb, x, meta, wf, wo, g_post, g_ffn):
    bn = x.shape[0]
    nt = L // TM
    blk16 = TM // HALO
    const = lambda b, i: (0, 0)
    return pl.pallas_call(
        _mix_out_body,
        grid=(bn, nt),
        in_specs=[
            pl.BlockSpec((1, NG, TM, GD), lambda b, i: (b, 0, i, 0)),
            pl.BlockSpec((1, TM, D), lambda b, i: (b, i, 0)),
            pl.BlockSpec((1, TM, D), lambda b, i: (b, i, 0)),
            pl.BlockSpec((1, HALO, D), lambda b, i: (b, jnp.maximum(i * blk16 - 1, 0), 0)),
            pl.BlockSpec((1, TM, D), lambda b, i: (b, i, 0)),
            pl.BlockSpec((N_META, D), const),
            pl.BlockSpec((FW, D), const, pipeline_mode=pl.Buffered(1)),
            pl.BlockSpec((D, D), const, pipeline_mode=pl.Buffered(1)),
            pl.BlockSpec((1, D), const),
            pl.BlockSpec((1, D), const),
        ],
        out_specs=[
            pl.BlockSpec((1, TM, D), lambda b, i: (b, i, 0)),
            pl.BlockSpec((1, TM, D), lambda b, i: (b, i, 0)),
        ],
        out_shape=[
            jax.ShapeDtypeStruct((bn, L, D), F32),
            jax.ShapeDtypeStruct((bn, L, D), BF),
        ],
        compiler_params=pltpu.CompilerParams(
            dimension_semantics=("arbitrary", "arbitrary"), vmem_limit_bytes=VMEM_LIMIT),
        name="mix_out",
    )(y, g0, mb, x, x, meta, wf, wo, g_post, g_ffn)


def _ffn_body(h2m_ref, h2n_ref, hrm_ref, hrn_ref, wa_ref, wv_ref, wd_ref, cwa_ref, cwv_ref, cba_ref, cbv_ref,
              gpost_ref, o_ref, hs_ref, ua0_ref, ua1_ref, uv0_ref, uv1_ref, act0_ref, act1_ref, acc_ref):
    ua_refs, uv_refs, act_refs = (ua0_ref, ua1_ref), (uv0_ref, uv1_ref), (act0_ref, act1_ref)
    j = pl.program_id(1)
    hs_ref[0:TF, :] = h2m_ref[0]
    pos = (j + 1) * TF + jax.lax.broadcasted_iota(jnp.int32, (FH, 1), 0)
    hs_ref[TF:, :] = jnp.where(pos < L, h2n_ref[0], jnp.zeros((), BF))

    def up(c, slot, p):
        r0, r1 = UP_ROWS[p], UP_ROWS[p + 1]
        hs = hs_ref[r0:r1, :]
        for u_ref, w_ref in ((ua_refs[slot], wa_ref), (uv_refs[slot], wv_ref)):
            r = _bdot(hs, w_ref[c])
            for s in range(FC // GD):
                u_ref[s, r0:r1, :] = r[:, s * GD:(s + 1) * GD]

    def conv(u_ref, s, r0, cw_ref, cb_ref, c):
        lanes = slice(s * GD, (s + 1) * GD)
        return (cb_ref[c, :, lanes]
                + u_ref[s, pl.ds(r0 + N_META - 1, FR), :] * cw_ref[c, 0:1, lanes]
                + u_ref[s, pl.ds(r0 + N_META, FR), :] * cw_ref[c, 1:2, lanes]
                + u_ref[s, pl.ds(r0 + N_META + 1, FR), :] * cw_ref[c, 2:3, lanes])

    def elem(c, slot, p):
        for rc in range(FP // FR):
            r0 = p * FP + rc * FR
            for s in range(FC // GD):
                a = conv(ua_refs[slot], s, r0, cwa_ref, cba_ref, c)
                v = conv(uv_refs[slot], s, r0, cwv_ref, cbv_ref, c)
                act_refs[slot][r0:r0 + FR, s * GD:(s + 1) * GD] = (jax.nn.gelu(a) * v).astype(BF)

    def down(c, slot, p):
        rows = slice(p * FP, (p + 1) * FP)
        acc_ref[rows, :] += _bdot(act_refs[slot][rows, :], wd_ref[c])

    def stage(c, slot, do_up=True, do_elem=True, do_down=True):
        @pl.when((j < SEQ // TF) if slot else (j >= 0))
        def _():
            for p in range(NP):
                if do_up:
                    up(c + 1, 1 - slot, p)
                if do_elem:
                    elem(c, slot, p)
                if do_down:
                    down(c - 1, 1 - slot, p)

    acc_ref[...] = jnp.zeros_like(acc_ref)
    stage(-1, 1, do_elem=False, do_down=False)
    stage(0, 0, do_down=False)

    def steady(k, carry):
        c = 2 * k + 1
        stage(c, 1)
        stage(c + 1, 0)
        return carry

    jax.lax.fori_loop(0, (NFC - 3) // 2, steady, 0)
    stage(NFC - 2, 1)
    stage(NFC - 1, 0, do_up=False)
    stage(NFC, 1, do_up=False, do_elem=False)

    nrm = _rms(acc_ref[...], gpost_ref[...])
    o_ref[0, 0:TF - N_META, :] = hrm_ref[0, N_META:, :] + nrm[0:TF - N_META]
    o_ref[0, TF - N_META:, :] = hrn_ref[0] + nrm[TF - N_META:]


def _ffn(h2, hres, wa, wv, wd, cwa, cwv, cba, cbv, g_post):
    bn = h2.shape[0]
    nt = SEQ // TF
    c2 = lambda b, j: (0, 0)
    c3 = lambda b, j: (0, 0, 0)
    return pl.pallas_call(
        _ffn_body,
        grid=(bn, nt),
        in_specs=[
            pl.BlockSpec((1, TF, D), lambda b, j: (b, j, 0)),
            pl.BlockSpec((1, FH, D), lambda b, j: (b, (j + 1) * (TF // FH), 0)),
            pl.BlockSpec((1, TF, D), lambda b, j: (b, j, 0)),
            pl.BlockSpec((1, N_META, D), lambda b, j: (b, (j + 1) * (TF // N_META), 0)),
            pl.BlockSpec((NFC, D, FC), c3, pipeline_mode=pl.Buffered(1)),
            pl.BlockSpec((NFC, D, FC), c3, pipeline_mode=pl.Buffered(1)),
            pl.BlockSpec((NFC, FC, D), c3, pipeline_mode=pl.Buffered(1)),
            pl.BlockSpec((NFC, 3, FC), c3),
            pl.BlockSpec((NFC, 3, FC), c3),
            pl.BlockSpec((NFC, 1, FC), c3),
            pl.BlockSpec((NFC, 1, FC), c3),
            pl.BlockSpec((1, D), c2),
        ],
        out_specs=pl.BlockSpec((1, TF, D), lambda b, j: (b, j, 0)),
        out_shape=jax.ShapeDtypeStruct((bn, SEQ, D), F32),
        scratch_shapes=[
            pltpu.VMEM((TF + FH, D), BF),
            pltpu.VMEM((FC // GD, TF + FH, GD), F32),
            pltpu.VMEM((FC // GD, TF + FH, GD), F32),
            pltpu.VMEM((FC // GD, TF + FH, GD), F32),
            pltpu.VMEM((FC // GD, TF + FH, GD), F32),
            pltpu.VMEM((TF, FC), BF),
            pltpu.VMEM((TF, FC), BF),
            pltpu.VMEM((TF, D), F32),
        ],
        compiler_params=pltpu.CompilerParams(
            dimension_semantics=("arbitrary", "arbitrary"), vmem_limit_bytes=VMEM_LIMIT),
        name="ffn",
    )(h2, h2, hres, hres, wa, wv, wd, cwa, cwv, cba, cbv, g_post)


def _chunk_cols(w):
    return w.reshape(w.shape[0], NFC, FC).transpose(1, 0, 2)


def kernel(x, meta_tokens, g_mix_pre, w_in, b_gates, w_fourier, conv_w_mix, conv_b_mix, w_conv_out, w_out,
           g_mix_post, g_ffn_pre, w_up, conv_w_ffn, conv_b_ffn, w_down, g_ffn_post):
    assert x.shape[1:] == (SEQ, D) and w_in.shape[0] == 1
    m1, bdc, bds = _seq_fft_tables()
    cs = jnp.asarray(_channel_dft_table())

    w_pq = _fold(w_in[0, :, :FW], cs)
    w_ext = jnp.concatenate([w_pq, w_in[0, :, FW:]], axis=1).astype(BF)

    z, g0, mb = _mix_in(x, meta_tokens, g_mix_pre, w_ext, b_gates, conv_w_mix[0], conv_b_mix,
                        w_conv_out[0].astype(BF))
    y = _seqfft(z, jnp.asarray(m1).astype(BF), jnp.asarray(bdc).astype(BF), jnp.asarray(bds).astype(BF))
    hres, h2 = _mix_out(y, g0, mb, x, meta_tokens, w_fourier[0].astype(BF), w_out[0].astype(BF),
                        g_mix_post, g_ffn_pre)

    wu = w_up[0].astype(BF)
    cwf = conv_w_ffn[0]
    cbf = conv_b_ffn
    return _ffn(h2, hres,
                _chunk_cols(wu[:, :D_FF]), _chunk_cols(wu[:, D_FF:]),
                w_down[0].astype(BF).reshape(NFC, FC, D),
                _chunk_cols(cwf[:, :D_FF]), _chunk_cols(cwf[:, D_FF:]),
                _chunk_cols(cbf[:, :D_FF]), _chunk_cols(cbf[:, D_FF:]),
                g_ffn_post)
```
